```python
import math
import jax, jax.numpy as jnp
from jax import lax
import numpy as np

D_MODEL = 2048
BATCH = 16
SEQ = 256
DEPTH = 4
DEC_BATCH = 2
DEC_SEQ = 2048
PAST_LEN = 512

GRID_W = 64
N_DIR = 2
D_SSM = 2048
SSM_HEAD_DIM = 64
SSM_HEADS = D_SSM // SSM_HEAD_DIM
SSM_STATE = 128
SSM_GROUPS = 4
SSM_CHUNK = 128
D_LRU = 2048
LRU_BLOCKS = 16
LRU_BLOCK_W = D_LRU // LRU_BLOCKS
LRU_C = 8.0
CONV_K = 4
CONV_PAD_L = CONV_K // 2
CONV_PAD_R = CONV_K - 1 - CONV_PAD_L
D_MIX = D_SSM + D_LRU
CONV_DIM = D_SSM + 2 * SSM_GROUPS * SSM_STATE
SPLIT_POINTS = (D_SSM, D_SSM + CONV_DIM, D_SSM + CONV_DIM + N_DIR * SSM_HEADS,
                D_SSM + CONV_DIM + N_DIR * SSM_HEADS + D_LRU)
IN_COLS = SPLIT_POINTS[-1] + D_LRU
N_EXPERTS = 16
N_EXPERT_GROUPS = 4
EXPERTS_PER_GROUP = N_EXPERTS // N_EXPERT_GROUPS
TOP_K = 2
D_EXPERT = 512
ALPHA = (2 * DEPTH) ** 0.25
BETA = (8 * DEPTH) ** -0.25
LN_EPS = 1e-5

kernel_name = "hymba_ssd_rglru_moe_dit_step"


def _layernorm(x, g=None, b=None):
    xf = x.astype(jnp.float32)
    mu = xf.mean(-1, keepdims=True)
    var = jnp.square(xf - mu).mean(-1, keepdims=True)
    y = (xf - mu) * lax.rsqrt(var + LN_EPS)
    if g is not None:
        y = y * g.astype(jnp.float32) + b.astype(jnp.float32)
    return y.astype(x.dtype)


def _rmsnorm_groups(y, g, n_groups):
    bsz, L, d = y.shape
    yf = y.astype(jnp.float32).reshape(bsz, L, n_groups, d // n_groups)
    yf = yf * lax.rsqrt(jnp.mean(jnp.square(yf), -1, keepdims=True) + LN_EPS)
    return (yf.reshape(bsz, L, d) * g.astype(jnp.float32)).astype(y.dtype)


def _dwconv(x, w, bias):
    L = x.shape[1]
    xp = jnp.pad(x, ((0, 0), (CONV_PAD_L, CONV_PAD_R), (0, 0)))
    y = xp[:, 0:L] * w[0]
    for k in range(1, CONV_K):
        y = y + xp[:, k:k + L] * w[k]
    return y + bias


def _to_col_major(h):
    bsz, L, d = h.shape
    rows = L // GRID_W
    return h.reshape(bsz, rows, GRID_W, d).swapaxes(1, 2).reshape(bsz, L, d)


def _to_row_major(h):
    bsz, L, d = h.shape
    rows = L // GRID_W
    return h.reshape(bsz, GRID_W, rows, d).swapaxes(1, 2).reshape(bsz, L, d)


def _ssd_scan(x, dt, a, bm, cm, h0):
    bsz, L, H, P = x.shape
    nc = L // SSM_CHUNK
    rep = H // SSM_GROUPS
    x = x.reshape(bsz, nc, SSM_CHUNK, H, P)
    dt = dt.reshape(bsz, nc, SSM_CHUNK, H)
    bm = jnp.repeat(bm, rep, axis=2).reshape(bsz, nc, SSM_CHUNK, H, SSM_STATE)
    cm = jnp.repeat(cm, rep, axis=2).reshape(bsz, nc, SSM_CHUNK, H, SSM_STATE)
    cs = jnp.cumsum(dt * a, axis=2)
    seg = cs[:, :, :, None, :] - cs[:, :, None, :, :]
    lower = jnp.tril(jnp.ones((SSM_CHUNK, SSM_CHUNK), bool))[None, None, :, :, None]
    decay = jnp.exp(jnp.where(lower, seg, -jnp.inf))
    xdt = x * dt[..., None]
    scores = jnp.einsum('bclhn,bcshn->bclsh', cm, bm)
    y_diag = jnp.einsum('bclsh,bcshp->bclhp', scores * decay, xdt)
    to_end = jnp.exp(cs[:, :, -1:, :] - cs)
    chunk_states = jnp.einsum('bcshn,bcsh,bcshp->bchpn', bm, to_end, xdt)
    chunk_decay = jnp.exp(cs[:, :, -1, :])

    def step(h, inp):
        st, dec = inp
        return h * dec[:, :, None, None] + st, h

    h_last, h_enter = lax.scan(step, h0.astype(x.dtype),
                               (jnp.moveaxis(chunk_states, 1, 0), jnp.moveaxis(chunk_decay, 1, 0)))
    h_enter = jnp.moveaxis(h_enter, 0, 1)
    y_off = jnp.einsum('bclhn,bchpn->bclhp', cm * jnp.exp(cs)[..., None], h_enter)
    return (y_diag + y_off).reshape(bsz, L, H, P), h_last


def _rglru(x, w_gate, b_gate, lam, h0):
    bsz, L, d = x.shape
    xb = x.reshape(bsz, L, LRU_BLOCKS, LRU_BLOCK_W)
    gates = jnp.einsum('blnj,gnjk->gblnk', xb, w_gate).reshape(2, bsz, L, d) + b_gate[:, None, None, :]
    r = jax.nn.sigmoid(gates[0])
    i = jax.nn.sigmoid(gates[1])
    log_a = -LRU_C * r * jax.nn.softplus(-lam)
    a = jnp.exp(log_a)
    u = jnp.sqrt(-jnp.expm1(2.0 * log_a)) * (i * x)
    u = u.at[:, 0].add(a[:, 0] * h0.astype(u.dtype))

    def combine(left, right):
        a1, b1 = left
        a2, b2 = right
        return a1 * a2, a2 * b1 + b2

    _, h = lax.associative_scan(combine, (a, u), axis=1)
    return h, h[:, -1]


def _token_mixer(h, w_in, conv_ssm_w, conv_ssm_b, ssm_a_log, ssm_dt_bias, ssm_d, ssm_norm_g,
                 conv_lru_w, conv_lru_b, lru_w_gate, lru_b_gate, lru_lambda, lru_norm_g, w_out,
                 ssm_h0, lru_h0):
    bsz, L, _ = h.shape
    proj = h @ w_in
    z, xbc, dt_raw, y_gate, x_lru = jnp.split(proj, SPLIT_POINTS, axis=-1)
    xbc = jax.nn.silu(_dwconv(xbc, conv_ssm_w, conv_ssm_b))
    xs, bm, cm = jnp.split(xbc, (D_SSM, D_SSM + SSM_GROUPS * SSM_STATE), axis=-1)
    xs = xs.reshape(bsz, L, SSM_HEADS, SSM_HEAD_DIM)
    bm = bm.reshape(bsz, L, SSM_GROUPS, SSM_STATE)
    cm = cm.reshape(bsz, L, SSM_GROUPS, SSM_STATE)
    dt_raw = dt_raw.reshape(bsz, L, N_DIR, SSM_HEADS)
    x_lru = _dwconv(x_lru, conv_lru_w, conv_lru_b)
    ssm_parts, ssm_fin, lru_parts, lru_fin = [], [], [], []
    for d in range(N_DIR):
        flip = (lambda t: jnp.flip(t, axis=1)) if d == 1 else (lambda t: t)
        dt = jax.nn.softplus(dt_raw[:, :, d] + ssm_dt_bias[d])
        a = -jnp.exp(ssm_a_log[d])
        y_d, hs = _ssd_scan(flip(xs), flip(dt), a, flip(bm), flip(cm), ssm_h0[:, d])
        ssm_parts.append(flip(y_d) + ssm_d[d][:, None] * xs)
        ssm_fin.append(hs)
        hl, hl_fin = _rglru(flip(x_lru), lru_w_gate[d], lru_b_gate[d], lru_lambda[d], lru_h0[:, d])
        lru_parts.append(flip(hl))
        lru_fin.append(hl_fin)
    y_ssm = (ssm_parts[0] + ssm_parts[1]).reshape(bsz, L, D_SSM) * jax.nn.silu(z)
    y_ssm = _rmsnorm_groups(y_ssm, ssm_norm_g, SSM_GROUPS)
    y_lru = _rmsnorm_groups(jax.nn.gelu(y_gate) * (lru_parts[0] + lru_parts[1]), lru_norm_g, 1)
    out = jnp.concatenate([y_ssm, y_lru], axis=-1) @ w_out
    return out, jnp.stack(ssm_fin, axis=1), jnp.stack(lru_fin, axis=1)


def _moe(h, router_w, router_b, w_gate, w_up, w_down):
    bsz, L, d = h.shape
    t = h.reshape(-1, d)
    scores = jax.nn.softmax((t @ router_w).astype(jnp.float32), axis=-1)
    sel = scores + router_b.astype(jnp.float32)
    grp = sel.reshape(-1, N_EXPERT_GROUPS, EXPERTS_PER_GROUP)
    grp_score = lax.top_k(grp, TOP_K)[0].sum(-1)
    best = jnp.argmax(grp_score, axis=-1)
    in_group = (jnp.arange(N_EXPERTS) // EXPERTS_PER_GROUP)[None, :] == best[:, None]
    _, idx = lax.top_k(jnp.where(in_group, sel, -jnp.inf), TOP_K)
    w = jnp.take_along_axis(scores, idx, axis=-1)
    w = w / w.sum(-1, keepdims=True)
    combine = jnp.einsum('tk,tke->te', w, jax.nn.one_hot(idx, N_EXPERTS, dtype=jnp.float32)).astype(h.dtype)
    hid = jax.nn.silu(jnp.einsum('td,edf->tef', t, w_gate)) * jnp.einsum('td,edf->tef', t, w_up)
    out = jnp.einsum('tef,efd->td', hid * combine[:, :, None], w_down)
    return out.reshape(bsz, L, d)


def _layer(x, mod, grid_col_major, lp, moe_p, ssm_h0, lru_h0):
    (w_in, conv_ssm_w, conv_ssm_b, ssm_a_log, ssm_dt_bias, ssm_d, ssm_norm_g, conv_lru_w, conv_lru_b,
     lru_w_gate, lru_b_gate, lru_lambda, lru_norm_g, w_out, ln_mix_g, ln_mix_b, w_gate, w_up, w_down,
     ln_ffn_g, ln_ffn_b) = lp
    router_w, router_b = moe_p
    shift1, scale1, gate1, shift2, scale2, gate2 = jnp.split(mod, 6, axis=-1)
    h = _layernorm(x) * (1.0 + scale1) + shift1
    if grid_col_major:
        h = _to_col_major(h)
    m, ssm_f, lru_f = _token_mixer(h, w_in, conv_ssm_w, conv_ssm_b, ssm_a_log, ssm_dt_bias, ssm_d, ssm_norm_g,
                                   conv_lru_w, conv_lru_b, lru_w_gate, lru_b_gate, lru_lambda, lru_norm_g,
                                   w_out, ssm_h0, lru_h0)
    if grid_col_major:
        m = _to_row_major(m)
    x = _layernorm(ALPHA * x + gate1 * m, ln_mix_g, ln_mix_b)
    h = _layernorm(x) * (1.0 + scale2) + shift2
    f = _moe(h, router_w, router_b, w_gate, w_up, w_down)
    x = _layernorm(ALPHA * x + gate2 * f, ln_ffn_g, ln_ffn_b)
    return x, ssm_f, lru_f


def setup_inputs(seed: int = 0) -> dict:
    key = jax.random.key(seed)
    ks = jax.random.split(key, 40)
    f32 = jnp.float32

    def nrm(k, shape, scale):
        return jax.random.normal(k, shape, f32) * scale

    def gain(k, shape):
        return 1.0 + 0.02 * jax.random.normal(k, shape, f32)

    dt0 = jnp.exp(jax.random.uniform(ks[10], (DEPTH, N_DIR, SSM_HEADS), f32, math.log(1e-3), math.log(1e-1)))
    a8 = jax.random.uniform(ks[17], (DEPTH, N_DIR, D_LRU), f32, 0.9, 0.999)
    sig = a8 ** (1.0 / LRU_C)
    return {
        "x_prompt": nrm(ks[0], (BATCH, SEQ, D_MODEL), 1.0),
        "x_sample": nrm(ks[1], (DEC_BATCH, DEC_SEQ, D_MODEL), 1.0),
        "state_ssm": nrm(ks[2], (DEC_BATCH, DEPTH, N_DIR, SSM_HEADS, SSM_HEAD_DIM, SSM_STATE), 0.1),
        "state_lru": nrm(ks[3], (DEC_BATCH, DEPTH, N_DIR, D_LRU), 0.5),
        "c": nrm(ks[4], (DEC_BATCH, D_MODEL), 1.0),
        "c_ctx": nrm(ks[5], (D_MODEL,), 1.0),
        "w_ada": nrm(ks[6], (DEPTH, D_MODEL, 6 * D_MODEL), 0.5 * D_MODEL ** -0.5),
        "b_ada": nrm(ks[7], (DEPTH, 6 * D_MODEL), 0.02),
        "w_in": nrm(ks[8], (DEPTH, D_MODEL, IN_COLS), D_MODEL ** -0.5),
        "conv_ssm_w": nrm(ks[9], (DEPTH, CONV_K, CONV_DIM), CONV_K ** -0.5),
        "conv_ssm_b": nrm(ks[11], (DEPTH, CONV_DIM), 0.02),
        "ssm_a_log": jnp.log(jax.random.uniform(ks[12], (DEPTH, N_DIR, SSM_HEADS), f32, 1.0, 16.0)),
        "ssm_dt_bias": dt0 + jnp.log(-jnp.expm1(-dt0)),
        "ssm_d": gain(ks[13], (DEPTH, N_DIR, SSM_HEADS)),
        "ssm_norm_g": gain(ks[14], (DEPTH, D_SSM)),
        "conv_lru_w": nrm(ks[15], (DEPTH, CONV_K, D_LRU), CONV_K ** -0.5),
        "conv_lru_b": nrm(ks[16], (DEPTH, D_LRU), 0.02),
        "lru_w_gate": nrm(ks[18], (DEPTH, N_DIR, 2, LRU_BLOCKS, LRU_BLOCK_W, LRU_BLOCK_W), LRU_BLOCK_W ** -0.5),
        "lru_b_gate": nrm(ks[19], (DEPTH, N_DIR, 2, D_LRU), 0.02),
        "lru_lambda": jnp.log(sig / (1.0 - sig)),
        "lru_norm_g": gain(ks[20], (DEPTH, D_LRU)),
        "w_out": nrm(ks[21], (DEPTH, D_MIX, D_MODEL), BETA * D_MIX ** -0.5),
        "ln_mix_g": gain(ks[22], (DEPTH, D_MODEL)),
        "ln_mix_b": nrm(ks[23], (DEPTH, D_MODEL), 0.02),
        "router_w": nrm(ks[24], (D_MODEL, N_EXPERTS), D_MODEL ** -0.5),
        "router_b": nrm(ks[25], (N_EXPERTS,), 0.01),
        "moe_w_gate": nrm(ks[26], (DEPTH, N_EXPERTS, D_MODEL, D_EXPERT), D_MODEL ** -0.5),
        "moe_w_up": nrm(ks[27], (DEPTH, N_EXPERTS, D_MODEL, D_EXPERT), D_MODEL ** -0.5),
        "moe_w_down": nrm(ks[28], (DEPTH, N_EXPERTS, D_EXPERT, D_MODEL), BETA * D_EXPERT ** -0.5),
        "ln_ffn_g": gain(ks[29], (DEPTH, D_MODEL)),
        "ln_ffn_b": nrm(ks[30], (DEPTH, D_MODEL), 0.02),
    }


def reference(x_prompt, x_sample, state_ssm, state_lru, c, c_ctx, w_ada, b_ada, w_in, conv_ssm_w, conv_ssm_b,
              ssm_a_log, ssm_dt_bias, ssm_d, ssm_norm_g, conv_lru_w, conv_lru_b, lru_w_gate, lru_b_gate,
              lru_lambda, lru_norm_g, w_out, ln_mix_g, ln_mix_b, router_w, router_b, moe_w_gate, moe_w_up,
              moe_w_down, ln_ffn_g, ln_ffn_b):
    n_req = x_prompt.shape[0]
    zero_ssm = jnp.zeros((n_req, N_DIR, SSM_HEADS, SSM_HEAD_DIM, SSM_STATE), x_prompt.dtype)
    zero_lru = jnp.zeros((n_req, N_DIR, D_LRU), x_prompt.dtype)
    moe_p = (router_w, router_b)
    yp, ys = x_prompt, x_sample
    ssm_states, lru_states = [], []
    for l in range(DEPTH):
        lp = (w_in[l], conv_ssm_w[l], conv_ssm_b[l], ssm_a_log[l], ssm_dt_bias[l], ssm_d[l], ssm_norm_g[l],
              conv_lru_w[l], conv_lru_b[l], lru_w_gate[l], lru_b_gate[l], lru_lambda[l], lru_norm_g[l], w_out[l],
              ln_mix_g[l], ln_mix_b[l], moe_w_gate[l], moe_w_up[l], moe_w_down[l], ln_ffn_g[l], ln_ffn_b[l])
        mod_ctx = (jax.nn.silu(c_ctx) @ w_ada[l] + b_ada[l])[None, None, :]
        mod_smp = (jax.nn.silu(c) @ w_ada[l] + b_ada[l])[:, None, :]
        yp, s_f, l_f = _layer(yp, mod_ctx, False, lp, moe_p, zero_ssm, zero_lru)
        ssm_states.append(s_f)
        lru_states.append(l_f)
        ys, _, _ = _layer(ys, mod_smp, l % 2 == 1, lp, moe_p, state_ssm[:, l], state_lru[:, l])
    new_state_ssm = jnp.stack(ssm_states, axis=1)
    new_state_lru = jnp.stack(lru_states, axis=1)
    return (yp, ys, new_state_ssm, new_state_lru)
```

```python
import jax
import jax.numpy as jnp
from jax import lax
from jax.experimental import pallas as pl
from jax.experimental.pallas import tpu as pltpu

F32 = jnp.float32
BF16 = jnp.bfloat16

D_MODEL = 2048
BATCH, SEQ = 16, 256
DEC_BATCH, DEC_SEQ = 2, 2048
DEPTH = 4
GRID_W = 64
N_DIR = 2
D_SSM = 2048
SSM_HEAD_DIM = 64
SSM_HEADS = 32
SSM_STATE = 128
SSM_GROUPS = 4
SSM_CHUNK = 128
D_LRU = 2048
LRU_BLOCKS = 16
LRU_BLOCK_W = 128
LRU_C = 8.0
CONV_K = 4
CONV_PAD_L = CONV_K // 2
CONV_PAD_R = CONV_K - 1 - CONV_PAD_L
D_MIX = D_SSM + D_LRU
CONV_DIM = D_SSM + 2 * SSM_GROUPS * SSM_STATE
SPLIT_POINTS = (D_SSM, D_SSM + CONV_DIM, D_SSM + CONV_DIM + N_DIR * SSM_HEADS,
                D_SSM + CONV_DIM + N_DIR * SSM_HEADS + D_LRU)
IN_COLS = SPLIT_POINTS[-1] + D_LRU
N_EXPERTS = 16
N_EXPERT_GROUPS = 4
EXPERTS_PER_GROUP = 4
TOP_K = 2
D_EXPERT = 512
ALPHA = (2 * DEPTH) ** 0.25
LN_EPS = 1e-5

T_PROMPT = BATCH * SEQ
T_SAMPLE = DEC_BATCH * DEC_SEQ
T_ALL = T_PROMPT + T_SAMPLE

LANES = 128
MIB = 1024 * 1024


def _cparams(sem, vmem_mib):
    return pltpu.CompilerParams(dimension_semantics=sem, vmem_limit_bytes=vmem_mib * MIB)


def _dot(a, b):
    return jnp.dot(a, b, preferred_element_type=F32)


MM_TM = 512
MM_TN = 1024
CAST_ROWS = 256


def _cast_block(src_ref, dst_ref):
    rows = src_ref.shape[0]

    def body(r, carry):
        sl = pl.ds(pl.multiple_of(r * CAST_ROWS, CAST_ROWS), CAST_ROWS)
        dst_ref[sl, :] = src_ref[sl, :].astype(BF16)
        return carry

    lax.fori_loop(0, rows // CAST_ROWS, body, 0)


def _mm_kernel(h_ref, w_ref, o_ref, wb_ref):
    @pl.when(pl.program_id(1) == 0)
    def _():
        _cast_block(w_ref, wb_ref)

    o_ref[...] = _dot(h_ref[...], wb_ref[...])


def _in_proj(h, w_in, l):
    return pl.pallas_call(
        _mm_kernel,
        out_shape=jax.ShapeDtypeStruct((T_ALL, IN_COLS), F32),
        grid=(pl.cdiv(IN_COLS, MM_TN), T_ALL // MM_TM),
        in_specs=[pl.BlockSpec((MM_TM, D_MODEL), lambda j, i: (i, 0)),
                  pl.BlockSpec((None, D_MODEL, MM_TN), lambda j, i: (l, 0, j))],
        out_specs=pl.BlockSpec((MM_TM, MM_TN), lambda j, i: (i, j)),
        scratch_shapes=[pltpu.VMEM((D_MODEL, MM_TN), BF16)],
        compiler_params=_cparams(("arbitrary", "arbitrary"), 44), name="in_proj",
    )(h, w_in)


OUT_TN = 512


def _out_proj_kernel(ys_ref, yl_ref, wt_ref, wb_ref, o_ref, wtb_ref, wbb_ref):
    @pl.when(pl.program_id(1) == 0)
    def _():
        _cast_block(wt_ref, wtb_ref)
        _cast_block(wb_ref, wbb_ref)

    o_ref[...] = _dot(ys_ref[...], wtb_ref[...]) + _dot(yl_ref[...], wbb_ref[...])


def _out_proj(y_ssm, y_lru, w_out, l):
    return pl.pallas_call(
        _out_proj_kernel,
        out_shape=jax.ShapeDtypeStruct((T_ALL, D_MODEL), F32),
        grid=(D_MODEL // OUT_TN, T_ALL // MM_TM),
        in_specs=[
            pl.BlockSpec((MM_TM, D_SSM), lambda j, i: (i, 0)),
            pl.BlockSpec((MM_TM, D_LRU), lambda j, i: (i, 0)),
            pl.BlockSpec((None, D_SSM, OUT_TN), lambda j, i: (l, 0, j)),
            pl.BlockSpec((None, D_LRU, OUT_TN), lambda j, i: (l, 1, j)),
        ],
        out_specs=pl.BlockSpec((MM_TM, OUT_TN), lambda j, i: (i, j)),
        scratch_shapes=[pltpu.VMEM((D_SSM, OUT_TN), BF16), pltpu.VMEM((D_LRU, OUT_TN), BF16)],
        compiler_params=_cparams(("arbitrary", "arbitrary"), 44), name="out_proj",
    )(y_ssm, y_lru, w_out, w_out)


MOE_TM = 1024
MOE_FB = 256


def _moe_kernel(h_ref, cw_ref, wg_ref, wu_ref, wd_ref, o_ref, wgb_ref, wub_ref, wdb_ref):
    e = pl.program_id(1)
    f = pl.program_id(2)

    @pl.when((e == 0) & (f == 0))
    def _():
        o_ref[...] = jnp.zeros_like(o_ref)

    _cast_block(wg_ref, wgb_ref)
    _cast_block(wu_ref, wub_ref)
    wdb_ref[...] = wd_ref[...].astype(BF16)
    h = h_ref[...]
    lane = lax.broadcasted_iota(jnp.int32, (MOE_TM, LANES), 1)
    comb = jnp.sum(jnp.where(lane == e, cw_ref[...], 0.0), axis=-1, keepdims=True)
    hg = _dot(h, wgb_ref[...])
    hid = hg * jax.nn.sigmoid(hg) * _dot(h, wub_ref[...]) * comb
    o_ref[...] += _dot(hid.astype(BF16), wdb_ref[...])


def _moe(h2, cw, w_gate, w_up, w_down, l):
    return pl.pallas_call(
        _moe_kernel,
        out_shape=jax.ShapeDtypeStruct((T_ALL, D_MODEL), F32),
        grid=(T_ALL // MOE_TM, N_EXPERTS, D_EXPERT // MOE_FB),
        in_specs=[
            pl.BlockSpec((MOE_TM, D_MODEL), lambda i, e, f: (i, 0)),
            pl.BlockSpec((MOE_TM, LANES), lambda i, e, f: (i, 0)),
            pl.BlockSpec((None, None, D_MODEL, MOE_FB), lambda i, e, f: (l, e, 0, f)),
            pl.BlockSpec((None, None, D_MODEL, MOE_FB), lambda i, e, f: (l, e, 0, f)),
            pl.BlockSpec((None, None, MOE_FB, D_MODEL), lambda i, e, f: (l, e, f, 0)),
        ],
        out_specs=pl.BlockSpec((MOE_TM, D_MODEL), lambda i, e, f: (i, 0)),
        scratch_shapes=[pltpu.VMEM((D_MODEL, MOE_FB), BF16), pltpu.VMEM((D_MODEL, MOE_FB), BF16),
                        pltpu.VMEM((MOE_FB, D_MODEL), BF16)],
        compiler_params=_cparams(("arbitrary", "arbitrary", "arbitrary"), 52), name="moe",
    )(h2, cw, w_gate, w_up, w_down)


def _layernorm(x, g=None, b=None):
    mu = x.mean(-1, keepdims=True)
    var = jnp.square(x - mu).mean(-1, keepdims=True)
    y = (x - mu) * lax.rsqrt(var + LN_EPS)
    if g is not None:
        y = y * g + b
    return y


def _rmsnorm_groups(y, g, n_groups):
    bsz, L, d = y.shape
    yf = y.reshape(bsz, L, n_groups, d // n_groups)
    yf = yf * lax.rsqrt(jnp.mean(jnp.square(yf), -1, keepdims=True) + LN_EPS)
    return yf.reshape(bsz, L, d) * g


def _dwconv(x, w, bias):
    L = x.shape[1]
    xp = jnp.pad(x, ((0, 0), (CONV_PAD_L, CONV_PAD_R), (0, 0)))
    y = xp[:, 0:L] * w[0]
    for k in range(1, CONV_K):
        y = y + xp[:, k:k + L] * w[k]
    return y + bias


def _to_col_major(h):
    bsz, L, d = h.shape
    return h.reshape(bsz, L // GRID_W, GRID_W, d).swapaxes(1, 2).reshape(bsz, L, d)


def _to_row_major(h):
    bsz, L, d = h.shape
    return h.reshape(bsz, GRID_W, L // GRID_W, d).swapaxes(1, 2).reshape(bsz, L, d)


def _ssd_scan(x, dt, a, bm, cm, h0):
    bsz, L, H, P = x.shape
    nc = L // SSM_CHUNK
    rep = H // SSM_GROUPS
    x = x.reshape(bsz, nc, SSM_CHUNK, H, P)
    dt = dt.reshape(bsz, nc, SSM_CHUNK, H)
    bm = jnp.repeat(bm, rep, axis=2).reshape(bsz, nc, SSM_CHUNK, H, SSM_STATE)
    cm = jnp.repeat(cm, rep, axis=2).reshape(bsz, nc, SSM_CHUNK, H, SSM_STATE)
    cs = jnp.cumsum(dt * a, axis=2)
    seg = cs[:, :, :, None, :] - cs[:, :, None, :, :]
    lower = jnp.tril(jnp.ones((SSM_CHUNK, SSM_CHUNK), bool))[None, None, :, :, None]
    decay = jnp.exp(jnp.where(lower, seg, -jnp.inf))
    xdt = x * dt[..., None]
    scores = jnp.einsum('bclhn,bcshn->bclsh', cm, bm)
    y_diag = jnp.einsum('bclsh,bcshp->bclhp', scores * decay, xdt)
    to_end = jnp.exp(cs[:, :, -1:, :] - cs)
    chunk_states = jnp.einsum('bcshn,bcsh,bcshp->bchpn', bm, to_end, xdt)
    chunk_decay = jnp.exp(cs[:, :, -1, :])

    def step(h, inp):
        st, dec = inp
        return h * dec[:, :, None, None] + st, h

    h_last, h_enter = lax.scan(step, h0, (jnp.moveaxis(chunk_states, 1, 0), jnp.moveaxis(chunk_decay, 1, 0)))
    h_enter = jnp.moveaxis(h_enter, 0, 1)
    y_off = jnp.einsum('bclhn,bchpn->bclhp', cm * jnp.exp(cs)[..., None], h_enter)
    return (y_diag + y_off).reshape(bsz, L, H, P), h_last


def _rglru(x, w_gate, b_gate, lam, h0):
    bsz, L, d = x.shape
    xb = x.reshape(bsz, L, LRU_BLOCKS, LRU_BLOCK_W)
    gates = jnp.einsum('blnj,gnjk->gblnk', xb, w_gate).reshape(2, bsz, L, d) + b_gate[:, None, None, :]
    r = jax.nn.sigmoid(gates[0])
    i = jax.nn.sigmoid(gates[1])
    log_a = -LRU_C * r * jax.nn.softplus(-lam)
    a = jnp.exp(log_a)
    u = jnp.sqrt(-jnp.expm1(2.0 * log_a)) * (i * x)
    u = u.at[:, 0].add(a[:, 0] * h0)

    def combine(left, right):
        a1, b1 = left
        a2, b2 = right
        return a1 * a2, a2 * b1 + b2

    _, h = lax.associative_scan(combine, (a, u), axis=1)
    return h, h[:, -1]


def _mixer_core(proj, p, l, ssm_h0, lru_h0):
    bsz, L, _ = proj.shape
    z, xbc, dt_raw, y_gate, x_lru = jnp.split(proj, SPLIT_POINTS, axis=-1)
    xbc = jax.nn.silu(_dwconv(xbc, p["conv_ssm_w"][l], p["conv_ssm_b"][l]))
    xs, bm, cm = jnp.split(xbc, (D_SSM, D_SSM + SSM_GROUPS * SSM_STATE), axis=-1)
    xs = xs.reshape(bsz, L, SSM_HEADS, SSM_HEAD_DIM)
    bm = bm.reshape(bsz, L, SSM_GROUPS, SSM_STATE)
    cm = cm.reshape(bsz, L, SSM_GROUPS, SSM_STATE)
    dt_raw = dt_raw.reshape(bsz, L, N_DIR, SSM_HEADS)
    x_lru = _dwconv(x_lru, p["conv_lru_w"][l], p["conv_lru_b"][l])
    ssm_parts, ssm_fin, lru_parts, lru_fin = [], [], [], []
    for d in range(N_DIR):
        flip = (lambda t: jnp.flip(t, axis=1)) if d == 1 else (lambda t: t)
        dt = jax.nn.softplus(dt_raw[:, :, d] + p["ssm_dt_bias"][l, d])
        a = -jnp.exp(p["ssm_a_log"][l, d])
        y_d, hs = _ssd_scan(flip(xs), flip(dt), a, flip(bm), flip(cm), ssm_h0[:, d])
        ssm_parts.append(flip(y_d) + p["ssm_d"][l, d][:, None] * xs)
        ssm_fin.append(hs)
        hl, hl_fin = _rglru(flip(x_lru), p["lru_w_gate"][l, d], p["lru_b_gate"][l, d],
                            p["lru_lambda"][l, d], lru_h0[:, d])
        lru_parts.append(flip(hl))
        lru_fin.append(hl_fin)
    y_ssm = (ssm_parts[0] + ssm_parts[1]).reshape(bsz, L, D_SSM) * jax.nn.silu(z)
    y_ssm = _rmsnorm_groups(y_ssm, p["ssm_norm_g"][l], SSM_GROUPS)
    y_lru = _rmsnorm_groups(jax.nn.gelu(y_gate) * (lru_parts[0] + lru_parts[1]), p["lru_norm_g"][l], 1)
    return y_ssm, y_lru, jnp.stack(ssm_fin, axis=1), jnp.stack(lru_fin, axis=1)


def _route(t, router_w, router_b):
    scores = jax.nn.softmax(t @ router_w, axis=-1)
    sel = scores + router_b
    grp = sel.reshape(-1, N_EXPERT_GROUPS, EXPERTS_PER_GROUP)
    grp_score = lax.top_k(grp, TOP_K)[0].sum(-1)
    best = jnp.argmax(grp_score, axis=-1)
    in_group = (jnp.arange(N_EXPERTS) // EXPERTS_PER_GROUP)[None, :] == best[:, None]
    _, idx = lax.top_k(jnp.where(in_group, sel, -jnp.inf), TOP_K)
    w = jnp.take_along_axis(scores, idx, axis=-1)
    w = w / w.sum(-1, keepdims=True)
    return jnp.einsum('tk,tke->te', w, jax.nn.one_hot(idx, N_EXPERTS, dtype=F32))


def _rows(xp, xs):
    return jnp.concatenate([xp.reshape(T_PROMPT, -1), xs.reshape(T_SAMPLE, -1)], axis=0)


def _unrows(x):
    return x[:T_PROMPT].reshape(BATCH, SEQ, -1), x[T_PROMPT:].reshape(DEC_BATCH, DEC_SEQ, -1)


def kernel(x_prompt, x_sample, state_ssm, state_lru, c, c_ctx, w_ada, b_ada, w_in, conv_ssm_w, conv_ssm_b, ssm_a_log, ssm_dt_bias, ssm_d, ssm_norm_g, conv_lru_w, conv_lru_b, lru_w_gate, lru_b_gate, lru_lambda, lru_norm_g, w_out, ln_mix_g, ln_mix_b, router_w, router_b, moe_w_gate, moe_w_up, moe_w_down, ln_ffn_g, ln_ffn_b):
    p = dict(conv_ssm_w=conv_ssm_w, conv_ssm_b=conv_ssm_b, ssm_a_log=ssm_a_log, ssm_dt_bias=ssm_dt_bias,
             ssm_d=ssm_d, ssm_norm_g=ssm_norm_g, conv_lru_w=conv_lru_w, conv_lru_b=conv_lru_b,
             lru_w_gate=lru_w_gate, lru_b_gate=lru_b_gate, lru_lambda=lru_lambda, lru_norm_g=lru_norm_g)
    zero_ssm = jnp.zeros((BATCH, N_DIR, SSM_HEADS, SSM_HEAD_DIM, SSM_STATE), F32)
    zero_lru = jnp.zeros((BATCH, N_DIR, D_LRU), F32)
    yp, ys = x_prompt, x_sample
    ssm_states, lru_states = [], []
    for l in range(DEPTH):
        col_major = (l % 2 == 1)
        mod_ctx = (jax.nn.silu(c_ctx) @ w_ada[l] + b_ada[l])[None, None, :]
        mod_smp = (jax.nn.silu(c) @ w_ada[l] + b_ada[l])[:, None, :]
        sh1p, sc1p, g1p, sh2p, sc2p, g2p = jnp.split(mod_ctx, 6, axis=-1)
        sh1s, sc1s, g1s, sh2s, sc2s, g2s = jnp.split(mod_smp, 6, axis=-1)

        hp = _layernorm(yp) * (1.0 + sc1p) + sh1p
        hs = _layernorm(ys) * (1.0 + sc1s) + sh1s
        if col_major:
            hs = _to_col_major(hs)
        proj = _in_proj(_rows(hp, hs).astype(BF16), w_in, l)
        proj_p, proj_s = _unrows(proj)

        ysm_p, ylr_p, s_f, l_f = _mixer_core(proj_p, p, l, zero_ssm, zero_lru)
        ysm_s, ylr_s, _, _ = _mixer_core(proj_s, p, l, state_ssm[:, l], state_lru[:, l])
        ssm_states.append(s_f)
        lru_states.append(l_f)
        m = _out_proj(_rows(ysm_p, ysm_s).astype(BF16), _rows(ylr_p, ylr_s).astype(BF16), w_out, l)
        mp, ms = _unrows(m)
        if col_major:
            ms = _to_row_major(ms)

        yp = _layernorm(ALPHA * yp + g1p * mp, ln_mix_g[l], ln_mix_b[l])
        ys = _layernorm(ALPHA * ys + g1s * ms, ln_mix_g[l], ln_mix_b[l])
        h2p = _layernorm(yp) * (1.0 + sc2p) + sh2p
        h2s = _layernorm(ys) * (1.0 + sc2s) + sh2s
        h2 = _rows(h2p, h2s)
        cw = jnp.pad(_route(h2, router_w, router_b), ((0, 0), (0, LANES - N_EXPERTS)))
        f = _moe(h2.astype(BF16), cw, moe_w_gate, moe_w_up, moe_w_down, l)
        fp, fs = _unrows(f)
        yp = _layernorm(ALPHA * yp + g2p * fp, ln_ffn_g[l], ln_ffn_b[l])
        ys = _layernorm(ALPHA * ys + g2s * fs, ln_ffn_g[l], ln_ffn_b[l])
    new_state_ssm = jnp.stack(ssm_states, axis=1)
    new_state_lru = jnp.stack(lru_states, axis=1)
    return (yp, ys, new_state_ssm, new_state_lru)
```

```python
import jax
import jax.numpy as jnp
from jax import lax
from jax.experimental import pallas as pl
from jax.experimental.pallas import tpu as pltpu

F32 = jnp.float32
BF16 = jnp.bfloat16

D_MODEL = 2048
BATCH, SEQ = 16, 256
DEC_BATCH, DEC_SEQ = 2, 2048
DEPTH = 4
GRID_W = 64
N_DIR = 2
D_SSM = 2048
SSM_HEAD_DIM = 64
SSM_HEADS = 32
SSM_STATE = 128
SSM_GROUPS = 4
SSM_CHUNK = 128
D_LRU = 2048
LRU_BLOCKS = 16
LRU_BLOCK_W = 128
LRU_C = 8.0
CONV_K = 4
CONV_PAD_L = CONV_K // 2
CONV_PAD_R = CONV_K - 1 - CONV_PAD_L
D_MIX = D_SSM + D_LRU
CONV_DIM = D_SSM + 2 * SSM_GROUPS * SSM_STATE
SPLIT_POINTS = (D_SSM, D_SSM + CONV_DIM, D_SSM + CONV_DIM + N_DIR * SSM_HEADS,
                D_SSM + CONV_DIM + N_DIR * SSM_HEADS + D_LRU)
IN_COLS = SPLIT_POINTS[-1] + D_LRU
N_EXPERTS = 16
N_EXPERT_GROUPS = 4
EXPERTS_PER_GROUP = 4
TOP_K = 2
D_EXPERT = 512
ALPHA = (2 * DEPTH) ** 0.25
LN_EPS = 1e-5

T_PROMPT = BATCH * SEQ
T_SAMPLE = DEC_BATCH * DEC_SEQ
T_ALL = T_PROMPT + T_SAMPLE

LANES = 128
MIB = 1024 * 1024


def _cparams(sem, vmem_mib):
    return pltpu.CompilerParams(dimension_semantics=sem, vmem_limit_bytes=vmem_mib * MIB)


def _dot(a, b):
    return jnp.dot(a, b, preferred_element_type=F32)


MM_TM = 512
MM_TN = 1024
CAST_ROWS = 256


def _cast_block(src_ref, dst_ref):
    rows = src_ref.shape[0]

    def body(r, carry):
        sl = pl.ds(pl.multiple_of(r * CAST_ROWS, CAST_ROWS), CAST_ROWS)
        dst_ref[sl, :] = src_ref[sl, :].astype(BF16)
        return carry

    lax.fori_loop(0, rows // CAST_ROWS, body, 0)


def _mm_kernel(h_ref, w_ref, o_ref, wb_ref):
    @pl.when(pl.program_id(1) == 0)
    def _():
        _cast_block(w_ref, wb_ref)

    o_ref[...] = _dot(h_ref[...], wb_ref[...])


def _in_proj(h, w_in, l):
    return pl.pallas_call(
        _mm_kernel,
        out_shape=jax.ShapeDtypeStruct((T_ALL, IN_COLS), F32),
        grid=(pl.cdiv(IN_COLS, MM_TN), T_ALL // MM_TM),
        in_specs=[pl.BlockSpec((MM_TM, D_MODEL), lambda j, i: (i, 0)),
                  pl.BlockSpec((None, D_MODEL, MM_TN), lambda j, i: (l, 0, j))],
        out_specs=pl.BlockSpec((MM_TM, MM_TN), lambda j, i: (i, j)),
        scratch_shapes=[pltpu.VMEM((D_MODEL, MM_TN), BF16)],
        compiler_params=_cparams(("arbitrary", "arbitrary"), 44), name="in_proj",
    )(h, w_in)


OUT_TN = 512


def _out_proj_kernel(ys_ref, yl_ref, wt_ref, wb_ref, o_ref, wtb_ref, wbb_ref):
    @pl.when(pl.program_id(1) == 0)
    def _():
        _cast_block(wt_ref, wtb_ref)
        _cast_block(wb_ref, wbb_ref)

    o_ref[...] = _dot(ys_ref[...], wtb_ref[...]) + _dot(yl_ref[...], wbb_ref[...])


def _out_proj(y_ssm, y_lru, w_out, l):
    return pl.pallas_call(
        _out_proj_kernel,
        out_shape=jax.ShapeDtypeStruct((T_ALL, D_MODEL), F32),
        grid=(D_MODEL // OUT_TN, T_ALL // MM_TM),
        in_specs=[
            pl.BlockSpec((MM_TM, D_SSM), lambda j, i: (i, 0)),
            pl.BlockSpec((MM_TM, D_LRU), lambda j, i: (i, 0)),
            pl.BlockSpec((None, D_SSM, OUT_TN), lambda j, i: (l, 0, j)),
            pl.BlockSpec((None, D_LRU, OUT_TN), lambda j, i: (l, 1, j)),
        ],
        out_specs=pl.BlockSpec((MM_TM, OUT_TN), lambda j, i: (i, j)),
        scratch_shapes=[pltpu.VMEM((D_SSM, OUT_TN), BF16), pltpu.VMEM((D_LRU, OUT_TN), BF16)],
        compiler_params=_cparams(("arbitrary", "arbitrary"), 44), name="out_proj",
    )(y_ssm, y_lru, w_out, w_out)


MOE_TM = 1024
MOE_FB = 256


def _moe_kernel(h_ref, cw_ref, wg_ref, wu_ref, wd_ref, o_ref, wgb_ref, wub_ref, wdb_ref):
    e = pl.program_id(1)
    f = pl.program_id(2)

    @pl.when((e == 0) & (f == 0))
    def _():
        o_ref[...] = jnp.zeros_like(o_ref)

    _cast_block(wg_ref, wgb_ref)
    _cast_block(wu_ref, wub_ref)
    wdb_ref[...] = wd_ref[...].astype(BF16)
    h = h_ref[...]
    lane = lax.broadcasted_iota(jnp.int32, (MOE_TM, LANES), 1)
    comb = jnp.sum(jnp.where(lane == e, cw_ref[...], 0.0), axis=-1, keepdims=True)
    hg = _dot(h, wgb_ref[...])
    hid = hg * jax.nn.sigmoid(hg) * _dot(h, wub_ref[...]) * comb
    o_ref[...] += _dot(hid.astype(BF16), wdb_ref[...])


def _moe(h2, cw, w_gate, w_up, w_down, l):
    return pl.pallas_call(
        _moe_kernel,
        out_shape=jax.ShapeDtypeStruct((T_ALL, D_MODEL), F32),
        grid=(T_ALL // MOE_TM, N_EXPERTS, D_EXPERT // MOE_FB),
        in_specs=[
            pl.BlockSpec((MOE_TM, D_MODEL), lambda i, e, f: (i, 0)),
            pl.BlockSpec((MOE_TM, LANES), lambda i, e, f: (i, 0)),
            pl.BlockSpec((None, None, D_MODEL, MOE_FB), lambda i, e, f: (l, e, 0, f)),
            pl.BlockSpec((None, None, D_MODEL, MOE_FB), lambda i, e, f: (l, e, 0, f)),
            pl.BlockSpec((None, None, MOE_FB, D_MODEL), lambda i, e, f: (l, e, f, 0)),
        ],
        out_specs=pl.BlockSpec((MOE_TM, D_MODEL), lambda i, e, f: (i, 0)),
        scratch_shapes=[pltpu.VMEM((D_MODEL, MOE_FB), BF16), pltpu.VMEM((D_MODEL, MOE_FB), BF16),
                        pltpu.VMEM((MOE_FB, D_MODEL), BF16)],
        compiler_params=_cparams(("arbitrary", "arbitrary", "arbitrary"), 52), name="moe",
    )(h2, cw, w_gate, w_up, w_down)


def _layernorm(x, g=None, b=None):
    mu = x.mean(-1, keepdims=True)
    var = jnp.square(x - mu).mean(-1, keepdims=True)
    y = (x - mu) * lax.rsqrt(var + LN_EPS)
    if g is not None:
        y = y * g + b
    return y


def _rmsnorm_groups(y, g, n_groups):
    bsz, L, d = y.shape
    yf = y.reshape(bsz, L, n_groups, d // n_groups)
    yf = yf * lax.rsqrt(jnp.mean(jnp.square(yf), -1, keepdims=True) + LN_EPS)
    return yf.reshape(bsz, L, d) * g


def _dwconv(x, w, bias):
    L = x.shape[1]
    xp = jnp.pad(x, ((0, 0), (CONV_PAD_L, CONV_PAD_R), (0, 0)))
    y = xp[:, 0:L] * w[0]
    for k in range(1, CONV_K):
        y = y + xp[:, k:k + L] * w[k]
    return y + bias


def _to_col_major(h):
    bsz, L, d = h.shape
    return h.reshape(bsz, L // GRID_W, GRID_W, d).swapaxes(1, 2).reshape(bsz, L, d)


def _to_row_major(h):
    bsz, L, d = h.shape
    return h.reshape(bsz, GRID_W, L // GRID_W, d).swapaxes(1, 2).reshape(bsz, L, d)


def _ssd_scan(x, dt, a, bm, cm, h0):
    bsz, L, H, P = x.shape
    nc = L // SSM_CHUNK
    rep = H // SSM_GROUPS
    x = x.reshape(bsz, nc, SSM_CHUNK, H, P)
    dt = dt.reshape(bsz, nc, SSM_CHUNK, H)
    bm = jnp.repeat(bm, rep, axis=2).reshape(bsz, nc, SSM_CHUNK, H, SSM_STATE)
    cm = jnp.repeat(cm, rep, axis=2).reshape(bsz, nc, SSM_CHUNK, H, SSM_STATE)
    cs = jnp.cumsum(dt * a, axis=2)
    seg = cs[:, :, :, None, :] - cs[:, :, None, :, :]
    lower = jnp.tril(jnp.ones((SSM_CHUNK, SSM_CHUNK), bool))[None, None, :, :, None]
    decay = jnp.exp(jnp.where(lower, seg, -jnp.inf))
    xdt = x * dt[..., None]
    scores = jnp.einsum('bclhn,bcshn->bclsh', cm, bm)
    y_diag = jnp.einsum('bclsh,bcshp->bclhp', scores * decay, xdt)
    to_end = jnp.exp(cs[:, :, -1:, :] - cs)
    chunk_states = jnp.einsum('bcshn,bcsh,bcshp->bchpn', bm, to_end, xdt)
    chunk_decay = jnp.exp(cs[:, :, -1, :])

    def step(h, inp):
        st, dec = inp
        return h * dec[:, :, None, None] + st, h

    h_last, h_enter = lax.scan(step, h0, (jnp.moveaxis(chunk_states, 1, 0), jnp.moveaxis(chunk_decay, 1, 0)))
    h_enter = jnp.moveaxis(h_enter, 0, 1)
    y_off = jnp.einsum('bclhn,bchpn->bclhp', cm * jnp.exp(cs)[..., None], h_enter)
    return (y_diag + y_off).reshape(bsz, L, H, P), h_last


def _rglru(x, w_gate, b_gate, lam, h0):
    bsz, L, d = x.shape
    xb = x.reshape(bsz, L, LRU_BLOCKS, LRU_BLOCK_W)
    gates = jnp.einsum('blnj,gnjk->gblnk', xb, w_gate).reshape(2, bsz, L, d) + b_gate[:, None, None, :]
    r = jax.nn.sigmoid(gates[0])
    i = jax.nn.sigmoid(gates[1])
    log_a = -LRU_C * r * jax.nn.softplus(-lam)
    a = jnp.exp(log_a)
    u = jnp.sqrt(-jnp.expm1(2.0 * log_a)) * (i * x)
    u = u.at[:, 0].add(a[:, 0] * h0)
    h = _linear_scan(a, u)
    return h, h[:, -1]


SCAN_CHUNK = 16


def _linear_scan(a, u):
    bsz, L, d = a.shape
    if L <= SCAN_CHUNK:
        hs = [u[:, 0]]
        for t in range(1, L):
            hs.append(a[:, t] * hs[-1] + u[:, t])
        return jnp.stack(hs, axis=1)
    n = L // SCAN_CHUNK
    a4 = a.reshape(bsz, n, SCAN_CHUNK, d)
    u4 = u.reshape(bsz, n, SCAN_CHUNK, d)
    h_loc = [u4[:, :, 0]]
    p_loc = [a4[:, :, 0]]
    for t in range(1, SCAN_CHUNK):
        h_loc.append(a4[:, :, t] * h_loc[-1] + u4[:, :, t])
        p_loc.append(a4[:, :, t] * p_loc[-1])
    h_end = _linear_scan(p_loc[-1], h_loc[-1])
    h_in = jnp.concatenate([jnp.zeros((bsz, 1, d), a.dtype), h_end[:, :-1]], axis=1)
    out = jnp.stack([h_loc[t] + p_loc[t] * h_in for t in range(SCAN_CHUNK)], axis=2)
    return out.reshape(bsz, L, d)


def _mixer_core(proj, p, l, ssm_h0, lru_h0):
    bsz, L, _ = proj.shape
    z, xbc, dt_raw, y_gate, x_lru = jnp.split(proj, SPLIT_POINTS, axis=-1)
    xbc = jax.nn.silu(_dwconv(xbc, p["conv_ssm_w"][l], p["conv_ssm_b"][l]))
    xs, bm, cm = jnp.split(xbc, (D_SSM, D_SSM + SSM_GROUPS * SSM_STATE), axis=-1)
    xs = xs.reshape(bsz, L, SSM_HEADS, SSM_HEAD_DIM)
    bm = bm.reshape(bsz, L, SSM_GROUPS, SSM_STATE)
    cm = cm.reshape(bsz, L, SSM_GROUPS, SSM_STATE)
    dt_raw = dt_raw.reshape(bsz, L, N_DIR, SSM_HEADS)
    x_lru = _dwconv(x_lru, p["conv_lru_w"][l], p["conv_lru_b"][l])
    ssm_parts, ssm_fin, lru_parts, lru_fin = [], [], [], []
    for d in range(N_DIR):
        flip = (lambda t: jnp.flip(t, axis=1)) if d == 1 else (lambda t: t)
        dt = jax.nn.softplus(dt_raw[:, :, d] + p["ssm_dt_bias"][l, d])
        a = -jnp.exp(p["ssm_a_log"][l, d])
        y_d, hs = _ssd_scan(flip(xs), flip(dt), a, flip(bm), flip(cm), ssm_h0[:, d])
        ssm_parts.append(flip(y_d) + p["ssm_d"][l, d][:, None] * xs)
        ssm_fin.append(hs)
        hl, hl_fin = _rglru(flip(x_lru), p["lru_w_gate"][l, d], p["lru_b_gate"][l, d],
                            p["lru_lambda"][l, d], lru_h0[:, d])
        lru_parts.append(flip(hl))
        lru_fin.append(hl_fin)
    y_ssm = (ssm_parts[0] + ssm_parts[1]).reshape(bsz, L, D_SSM) * jax.nn.silu(z)
    y_ssm = _rmsnorm_groups(y_ssm, p["ssm_norm_g"][l], SSM_GROUPS)
    y_lru = _rmsnorm_groups(jax.nn.gelu(y_gate) * (lru_parts[0] + lru_parts[1]), p["lru_norm_g"][l], 1)
    return y_ssm, y_lru, jnp.stack(ssm_fin, axis=1), jnp.stack(lru_fin, axis=1)


def _route(t, router_w, router_b):
    scores = jax.nn.softmax(t @ router_w, axis=-1)
    sel = scores + router_b
    grp = sel.reshape(-1, N_EXPERT_GROUPS, EXPERTS_PER_GROUP)
    grp_score = lax.top_k(grp, TOP_K)[0].sum(-1)
    best = jnp.argmax(grp_score, axis=-1)
    in_group = (jnp.arange(N_EXPERTS) // EXPERTS_PER_GROUP)[None, :] == best[:, None]
    _, idx = lax.top_k(jnp.where(in_group, sel, -jnp.inf), TOP_K)
    w = jnp.take_along_axis(scores, idx, axis=-1)
    w = w / w.sum(-1, keepdims=True)
    return jnp.einsum('tk,tke->te', w, jax.nn.one_hot(idx, N_EXPERTS, dtype=F32))


def _rows(xp, xs):
    return jnp.concatenate([xp.reshape(T_PROMPT, -1), xs.reshape(T_SAMPLE, -1)], axis=0)


def _unrows(x):
    return x[:T_PROMPT].reshape(BATCH, SEQ, -1), x[T_PROMPT:].reshape(DEC_BATCH, DEC_SEQ, -1)


def kernel(x_prompt, x_sample, state_ssm, state_lru, c, c_ctx, w_ada, b_ada, w_in, conv_ssm_w, conv_ssm_b, ssm_a_log, ssm_dt_bias, ssm_d, ssm_norm_g, conv_lru_w, conv_lru_b, lru_w_gate, lru_b_gate, lru_lambda, lru_norm_g, w_out, ln_mix_g, ln_mix_b, router_w, router_b, moe_w_gate, moe_w_up, moe_w_down, ln_ffn_g, ln_ffn_b):
    p = dict(conv_ssm_w=conv_ssm_w, conv_ssm_b=conv_ssm_b, ssm_a_log=ssm_a_log, ssm_dt_bias=ssm_dt_bias,
             ssm_d=ssm_d, ssm_norm_g=ssm_norm_g, conv_lru_w=conv_lru_w, conv_lru_b=conv_lru_b,
             lru_w_gate=lru_w_gate, lru_b_gate=lru_b_gate, lru_lambda=lru_lambda, lru_norm_g=lru_norm_g)
    zero_ssm = jnp.zeros((BATCH, N_DIR, SSM_HEADS, SSM_HEAD_DIM, SSM_STATE), F32)
    zero_lru = jnp.zeros((BATCH, N_DIR, D_LRU), F32)
    yp, ys = x_prompt, x_sample
    ssm_states, lru_states = [], []
    for l in range(DEPTH):
        col_major = (l % 2 == 1)
        mod_ctx = (jax.nn.silu(c_ctx) @ w_ada[l] + b_ada[l])[None, None, :]
        mod_smp = (jax.nn.silu(c) @ w_ada[l] + b_ada[l])[:, None, :]
        sh1p, sc1p, g1p, sh2p, sc2p, g2p = jnp.split(mod_ctx, 6, axis=-1)
        sh1s, sc1s, g1s, sh2s, sc2s, g2s = jnp.split(mod_smp, 6, axis=-1)

        hp = _layernorm(yp) * (1.0 + sc1p) + sh1p
        hs = _layernorm(ys) * (1.0 + sc1s) + sh1s
        if col_major:
            hs = _to_col_major(hs)
        proj = _in_proj(_rows(hp, hs).astype(BF16), w_in, l)
        proj_p, proj_s = _unrows(proj)

        ysm_p, ylr_p, s_f, l_f = _mixer_core(proj_p, p, l, zero_ssm, zero_lru)
        ysm_s, ylr_s, _, _ = _mixer_core(proj_s, p, l, state_ssm[:, l], state_lru[:, l])
        ssm_states.append(s_f)
        lru_states.append(l_f)
        m = _out_proj(_rows(ysm_p, ysm_s).astype(BF16), _rows(ylr_p, ylr_s).astype(BF16), w_out, l)
        mp, ms = _unrows(m)
        if col_major:
            ms = _to_row_major(ms)

        yp = _layernorm(ALPHA * yp + g1p * mp, ln_mix_g[l], ln_mix_b[l])
        ys = _layernorm(ALPHA * ys + g1s * ms, ln_mix_g[l], ln_mix_b[l])
        h2p = _layernorm(yp) * (1.0 + sc2p) + sh2p
        h2s = _layernorm(ys) * (1.0 + sc2s) + sh2s
        h2 = _rows(h2p, h2s)
        cw = jnp.pad(_route(h2, router_w, router_b), ((0, 0), (0, LANES - N_EXPERTS)))
        f = _moe(h2.astype(BF16), cw, moe_w_gate, moe_w_up, moe_w_down, l)
        fp, fs = _unrows(f)
        yp = _layernorm(ALPHA * yp + g2p * fp, ln_ffn_g[l], ln_ffn_b[l])
        ys = _layernorm(ALPHA * ys + g2s * fs, ln_ffn_g[l], ln_ffn_b[l])
    new_state_ssm = jnp.stack(ssm_states, axis=1)
    new_state_lru = jnp.stack(lru_states, axis=1)
    return (yp, ys, new_state_ssm, new_state_lru)
```
